```python
import jax, jax.numpy as jnp
from jax import lax
import numpy as np

D_MODEL = 1024
BATCH = 4
SEQ = 4096
DEPTH = 1

MIX_WIDTH = D_MODEL
CONV_WIDTH = MIX_WIDTH // 2
CONV_GROUPS = 8
CONV_KERNEL = 31
GM_WIDTH = MIX_WIDTH - CONV_WIDTH
GM_HEADS = 8
GM_HEAD_DIM = GM_WIDTH // GM_HEADS
CHUNK = 128
IN_COLS = 2 * CONV_WIDTH + 2 * GM_WIDTH
MEM_LEN = 256
XA_HEADS = 4
XA_HEAD_DIM = D_MODEL // XA_HEADS
FFN_HIDDEN = ((8 * D_MODEL // 3 + 255) // 256) * 256
RMS_EPS = 1e-6
LN_EPS = 1e-5

kernel_name = "hybrid_conv_gmlp_xattn_block"


def rmsnorm(x, g):
    xf = x.astype(jnp.float32)
    y = xf * lax.rsqrt(jnp.mean(xf * xf, axis=-1, keepdims=True) + RMS_EPS)
    return (y * g.astype(jnp.float32)).astype(x.dtype)


def layernorm(x, g, b):
    xf = x.astype(jnp.float32)
    mu = jnp.mean(xf, axis=-1, keepdims=True)
    var = jnp.mean(jnp.square(xf - mu), axis=-1, keepdims=True)
    y = (xf - mu) * lax.rsqrt(var + LN_EPS)
    return (y * g.astype(jnp.float32) + b.astype(jnp.float32)).astype(x.dtype)


def causal_depthwise_conv(a, w, b):
    k, c = w.shape
    a_pad = jnp.pad(a, ((0, 0), (k - 1, 0), (0, 0)))
    y = lax.conv_general_dilated(
        a_pad, w[:, None, :], window_strides=(1,), padding='VALID',
        dimension_numbers=('NWC', 'WIO', 'NWC'), feature_group_count=c)
    return y + b


def conformer_conv_group(za, zg, conv_w, conv_b, ln_g, ln_b):
    a = za * jax.nn.sigmoid(zg)
    a = causal_depthwise_conv(a, conv_w, conv_b)
    a = layernorm(a, ln_g, ln_b)
    return jax.nn.silu(a)


def gmlp_group(zu, zv, ln_g, ln_b, w_s, b_s):
    u = jax.nn.gelu(zu)
    v = layernorm(jax.nn.gelu(zv), ln_g, ln_b)
    bsz, s, _ = v.shape
    vh = v.reshape(bsz, s // CHUNK, CHUNK, GM_HEADS, GM_HEAD_DIM)
    mask = jnp.tril(jnp.ones((CHUNK, CHUNK), dtype=bool))
    ws = jnp.where(mask[None], w_s, jnp.zeros_like(w_s))
    mixed = jnp.einsum('hts,bnshd->bnthd', ws, vh)
    mixed = mixed + b_s.T[None, None, :, :, None]
    return u * mixed.reshape(bsz, s, GM_WIDTH)


def cross_attention(hn, mn, wq, wkv, wo):
    bsz, s, _ = hn.shape
    q = (hn @ wq).reshape(bsz, s, XA_HEADS, XA_HEAD_DIM)
    kv = mn @ wkv
    k, v = jnp.split(kv, 2, axis=-1)
    k = k.reshape(bsz, MEM_LEN, XA_HEADS, XA_HEAD_DIM)
    v = v.reshape(bsz, MEM_LEN, XA_HEADS, XA_HEAD_DIM)
    scale = XA_HEAD_DIM ** -0.5
    scores = jnp.einsum('bshd,bmhd->bhsm', q, k).astype(jnp.float32) * scale
    p = jax.nn.softmax(scores, axis=-1).astype(v.dtype)
    o = jnp.einsum('bhsm,bmhd->bshd', p, v).reshape(bsz, s, D_MODEL)
    return o @ wo


def swiglu(hn, w_gate_up, w_down):
    gu = hn @ w_gate_up
    g, u = jnp.split(gu, 2, axis=-1)
    return (jax.nn.silu(g) * u) @ w_down


def setup_inputs(seed: int = 0) -> dict:
    key = jax.random.key(seed)
    ks = jax.random.split(key, 24)
    f32 = jnp.float32

    def nrm(k, shape, scale):
        return jax.random.normal(k, shape, f32) * scale

    def gain(k, n):
        return jnp.ones((n,), f32) + 0.05 * jax.random.normal(k, (n,), f32)

    return {
        "x": nrm(ks[0], (BATCH, SEQ, D_MODEL), 1.0),
        "mem": nrm(ks[1], (BATCH, MEM_LEN, D_MODEL), 1.0),
        "norm_mix_g": gain(ks[2], D_MODEL),
        "w_in": nrm(ks[3], (D_MODEL, IN_COLS), D_MODEL ** -0.5),
        "b_in": nrm(ks[4], (IN_COLS,), 0.02),
        "conv_w": nrm(ks[5], (CONV_KERNEL, CONV_WIDTH), CONV_KERNEL ** -0.5),
        "conv_b": nrm(ks[6], (CONV_WIDTH,), 0.02),
        "conv_ln_g": gain(ks[7], CONV_WIDTH),
        "conv_ln_b": nrm(ks[8], (CONV_WIDTH,), 0.02),
        "gm_ln_g": gain(ks[9], GM_WIDTH),
        "gm_ln_b": nrm(ks[10], (GM_WIDTH,), 0.02),
        "gm_w_s": nrm(ks[11], (GM_HEADS, CHUNK, CHUNK), CHUNK ** -0.5),
        "gm_b_s": jnp.ones((GM_HEADS, CHUNK), f32) + 0.1 * jax.random.normal(ks[12], (GM_HEADS, CHUNK), f32),
        "w_out": nrm(ks[13], (MIX_WIDTH, D_MODEL), MIX_WIDTH ** -0.5),
        "norm_xa_g": gain(ks[14], D_MODEL),
        "mem_norm_g": gain(ks[15], D_MODEL),
        "xa_wq": nrm(ks[16], (D_MODEL, D_MODEL), D_MODEL ** -0.5),
        "xa_wkv": nrm(ks[17], (D_MODEL, 2 * D_MODEL), D_MODEL ** -0.5),
        "xa_wo": nrm(ks[18], (D_MODEL, D_MODEL), D_MODEL ** -0.5),
        "norm_ffn_g": gain(ks[19], D_MODEL),
        "ffn_w_gate_up": nrm(ks[20], (D_MODEL, 2 * FFN_HIDDEN), D_MODEL ** -0.5),
        "ffn_w_down": nrm(ks[21], (FFN_HIDDEN, D_MODEL), FFN_HIDDEN ** -0.5),
        "final_norm_g": gain(ks[22], D_MODEL),
    }


def reference(x, mem, norm_mix_g, w_in, b_in, conv_w, conv_b, conv_ln_g, conv_ln_b,
              gm_ln_g, gm_ln_b, gm_w_s, gm_b_s, w_out, norm_xa_g, mem_norm_g,
              xa_wq, xa_wkv, xa_wo, norm_ffn_g, ffn_w_gate_up, ffn_w_down,
              final_norm_g):
    h = x
    mn = rmsnorm(mem, mem_norm_g)
    for _ in range(DEPTH):
        hn = rmsnorm(h, norm_mix_g)
        z = hn @ w_in + b_in
        za, zg, zu, zv = jnp.split(
            z, [CONV_WIDTH, 2 * CONV_WIDTH, 2 * CONV_WIDTH + GM_WIDTH], axis=-1)
        conv_out = conformer_conv_group(za, zg, conv_w, conv_b, conv_ln_g, conv_ln_b)
        gm_out = gmlp_group(zu, zv, gm_ln_g, gm_ln_b, gm_w_s, gm_b_s)
        h = h + jnp.concatenate([conv_out, gm_out], axis=-1) @ w_out
        h = h + cross_attention(rmsnorm(h, norm_xa_g), mn, xa_wq, xa_wkv, xa_wo)
        h = h + swiglu(rmsnorm(h, norm_ffn_g), ffn_w_gate_up, ffn_w_down)
    return rmsnorm(h, final_norm_g)
```

```python
import functools

import jax
import jax.numpy as jnp
from jax import lax
from jax.experimental import pallas as pl
from jax.experimental.pallas import tpu as pltpu

D_MODEL = 1024
CONV_WIDTH = 512
CONV_KERNEL = 31
GM_WIDTH = 512
GM_HEADS = 8
GM_HEAD_DIM = 64
CHUNK = 128
MEM_LEN = 256
XA_HEADS = 4
XA_HEAD_DIM = 256
FFN_HIDDEN = 2816
RMS_EPS = 1e-6
LN_EPS = 1e-5

SEQ_TILE = 256
CONV_ROWS = 32
CARRY_ROWS = 32
FFN_COLS = 512
V7X_VMEM_LIMIT_BYTES = 58 * 1024 * 1024

F32 = jnp.float32
BF16 = jnp.bfloat16


def _rms(x, g):
    return x * lax.rsqrt(jnp.mean(x * x, axis=-1, keepdims=True) + RMS_EPS) * g


def _ln(x, g, b):
    mu = jnp.mean(x, axis=-1, keepdims=True)
    xc = x - mu
    var = jnp.mean(xc * xc, axis=-1, keepdims=True)
    return xc * lax.rsqrt(var + LN_EPS) * g + b


def _sigmoid(x):
    return 0.5 * (1.0 + jnp.tanh(0.5 * x))


def _gelu_tanh(x):
    c = 0.7978845608028654
    return 0.5 * x * (1.0 + jnp.tanh(c * (x + 0.044715 * (x * x * x))))


def _dot(a, b):
    return jnp.dot(a, b, preferred_element_type=F32)


def _kv_kernel(mem_ref, g_ref, wkv_ref, kt_ref, v_ref):
    mn = _rms(mem_ref[0], g_ref[...]).astype(BF16)
    kv = _dot(mn, wkv_ref[...])
    kt_ref[0] = (kv[:, :D_MODEL] * (XA_HEAD_DIM ** -0.5)).T.astype(BF16)
    v_ref[0] = kv[:, D_MODEL:].astype(BF16)


def _block_kernel(x_ref, kt_ref, v_ref,
                  g_mix_ref, w_in_ref, b_in_ref,
                  conv_w_ref, conv_b_ref, cln_g_ref, cln_b_ref,
                  gln_g_ref, gln_b_ref, ws_ref, bs_ref, w_out_ref,
                  g_xa_ref, wq_ref, wo_ref,
                  g_ffn_ref, wgu_ref, wd_ref, g_fin_ref,
                  o_ref,
                  aext_ref, mix_ref, att_ref, act_ref):
    T = SEQ_TILE
    j = pl.program_id(1)

    @pl.when(j == 0)
    def _():
        aext_ref[0:CARRY_ROWS, :] = jnp.zeros((CARRY_ROWS, CONV_WIDTH), F32)

    x = x_ref[0]
    hn = _rms(x, g_mix_ref[...]).astype(BF16)

    za = _dot(hn, w_in_ref[:, 0:CONV_WIDTH]) + b_in_ref[:, 0:CONV_WIDTH]
    zg = _dot(hn, w_in_ref[:, CONV_WIDTH:2 * CONV_WIDTH]) + b_in_ref[:, CONV_WIDTH:2 * CONV_WIDTH]
    aext_ref[CARRY_ROWS:CARRY_ROWS + T, :] = za * _sigmoid(zg)
    base = CARRY_ROWS - (CONV_KERNEL - 1)
    for r0 in range(0, T, CONV_ROWS):
        acc = jnp.broadcast_to(conv_b_ref[...], (CONV_ROWS, CONV_WIDTH))
        for k in range(CONV_KERNEL):
            acc = acc + conv_w_ref[k:k + 1, :] * aext_ref[r0 + base + k:r0 + base + k + CONV_ROWS, :]
        c = _ln(acc, cln_g_ref[...], cln_b_ref[...])
        mix_ref[r0:r0 + CONV_ROWS, 0:CONV_WIDTH] = (c * _sigmoid(c)).astype(BF16)
    aext_ref[0:CARRY_ROWS, :] = aext_ref[T:T + CARRY_ROWS, :]

    c0 = 2 * CONV_WIDTH
    zu = _dot(hn, w_in_ref[:, c0:c0 + GM_WIDTH]) + b_in_ref[:, c0:c0 + GM_WIDTH]
    zv = _dot(hn, w_in_ref[:, c0 + GM_WIDTH:c0 + 2 * GM_WIDTH]) + b_in_ref[:, c0 + GM_WIDTH:c0 + 2 * GM_WIDTH]
    u = _gelu_tanh(zu)
    v = _ln(_gelu_tanh(zv), gln_g_ref[...], gln_b_ref[...]).astype(BF16)
    trow = lax.broadcasted_iota(jnp.int32, (CHUNK, GM_HEADS * CHUNK), 0)
    scol = lax.broadcasted_iota(jnp.int32, (CHUNK, GM_HEADS * CHUNK), 1) % CHUNK
    ws = jnp.where(scol <= trow, ws_ref[...], jnp.zeros((), BF16))
    lane_head = lax.broadcasted_iota(jnp.int32, (CHUNK, GM_WIDTH), 1) // GM_HEAD_DIM
    for n in range(T // CHUNK):
        vc = v[n * CHUNK:(n + 1) * CHUNK]
        vstack = jnp.concatenate(
            [jnp.where(lane_head == h, vc, jnp.zeros((), BF16)) for h in range(GM_HEADS)], axis=0)
        mixed = _dot(ws, vstack) + bs_ref[...]
        mix_ref[n * CHUNK:(n + 1) * CHUNK, CONV_WIDTH:] = (u[n * CHUNK:(n + 1) * CHUNK] * mixed).astype(BF16)

    h1 = x + _dot(mix_ref[...], w_out_ref[...])

    q = _dot(_rms(h1, g_xa_ref[...]).astype(BF16), wq_ref[...]).astype(BF16)
    for h in range(XA_HEADS):
        sl = slice(h * XA_HEAD_DIM, (h + 1) * XA_HEAD_DIM)
        s = _dot(q[:, sl], kt_ref[0, sl, :])
        p = jnp.exp(s - jnp.max(s, axis=-1, keepdims=True))
        l = jnp.sum(p, axis=-1, keepdims=True)
        att_ref[:, sl] = (_dot(p.astype(BF16), v_ref[0, :, sl]) / l).astype(BF16)
    h2 = h1 + _dot(att_ref[...], wo_ref[...])

    hn3 = _rms(h2, g_ffn_ref[...]).astype(BF16)
    for f0 in range(0, FFN_HIDDEN, FFN_COLS):
        f1 = min(f0 + FFN_COLS, FFN_HIDDEN)
        g = _dot(hn3, wgu_ref[:, f0:f1])
        up = _dot(hn3, wgu_ref[:, FFN_HIDDEN + f0:FFN_HIDDEN + f1])
        act_ref[:, f0:f1] = (g * _sigmoid(g) * up).astype(BF16)
    h3 = h2 + _dot(act_ref[...], wd_ref[...])

    o_ref[0] = _rms(h3, g_fin_ref[...])


def _const_spec(shape):
    nd = len(shape)
    return pl.BlockSpec(shape, lambda b, j: (0,) * nd, pipeline_mode=pl.Buffered(1))


def kernel(x, mem, norm_mix_g, w_in, b_in, conv_w, conv_b, conv_ln_g, conv_ln_b, gm_ln_g, gm_ln_b, gm_w_s, gm_b_s, w_out, norm_xa_g, mem_norm_g, xa_wq, xa_wkv, xa_wo, norm_ffn_g, ffn_w_gate_up, ffn_w_down, final_norm_g):
    B, S, D = x.shape
    assert D == D_MODEL and S % SEQ_TILE == 0 and SEQ_TILE % CHUNK == 0
    row = lambda a: a.reshape(1, -1).astype(F32)

    kt, v = pl.pallas_call(
        _kv_kernel,
        grid=(B,),
        in_specs=[
            pl.BlockSpec((1, MEM_LEN, D), lambda b: (b, 0, 0)),
            pl.BlockSpec((1, D), lambda b: (0, 0)),
            pl.BlockSpec((D, 2 * D), lambda b: (0, 0)),
        ],
        out_specs=[
            pl.BlockSpec((1, D, MEM_LEN), lambda b: (b, 0, 0)),
            pl.BlockSpec((1, MEM_LEN, D), lambda b: (b, 0, 0)),
        ],
        out_shape=[
            jax.ShapeDtypeStruct((B, D, MEM_LEN), BF16),
            jax.ShapeDtypeStruct((B, MEM_LEN, D), BF16),
        ],
        name="kv_proj",
    )(mem, row(mem_norm_g), xa_wkv.astype(BF16))

    ws_cat = gm_w_s.transpose(1, 0, 2).reshape(CHUNK, GM_HEADS * CHUNK).astype(BF16)
    bs_full = jnp.repeat(gm_b_s.T, GM_HEAD_DIM, axis=1).astype(F32)
    conv_w_pad = jnp.pad(conv_w, ((0, 32 - CONV_KERNEL), (0, 0)))

    consts = [
        row(norm_mix_g), w_in.astype(BF16), row(b_in),
        conv_w_pad, row(conv_b), row(conv_ln_g), row(conv_ln_b),
        row(gm_ln_g), row(gm_ln_b), ws_cat, bs_full, w_out.astype(BF16),
        row(norm_xa_g), xa_wq.astype(BF16), xa_wo.astype(BF16),
        row(norm_ffn_g), ffn_w_gate_up.astype(BF16), ffn_w_down.astype(BF16), row(final_norm_g),
    ]
    T = SEQ_TILE
    return pl.pallas_call(
        _block_kernel,
        grid=(B, S // T),
        in_specs=[
            pl.BlockSpec((1, T, D), lambda b, j: (b, j, 0)),
            pl.BlockSpec((1, D, MEM_LEN), lambda b, j: (b, 0, 0)),
            pl.BlockSpec((1, MEM_LEN, D), lambda b, j: (b, 0, 0)),
        ] + [_const_spec(c.shape) for c in consts],
        out_specs=pl.BlockSpec((1, T, D), lambda b, j: (b, j, 0)),
        out_shape=jax.ShapeDtypeStruct((B, S, D), x.dtype),
        scratch_shapes=[
            pltpu.VMEM((T + CARRY_ROWS, CONV_WIDTH), F32),
            pltpu.VMEM((T, CONV_WIDTH + GM_WIDTH), BF16),
            pltpu.VMEM((T, D), BF16),
            pltpu.VMEM((T, FFN_HIDDEN), BF16),
        ],
        compiler_params=pltpu.CompilerParams(
            dimension_semantics=("arbitrary", "arbitrary"),
            vmem_limit_bytes=V7X_VMEM_LIMIT_BYTES,
        ),
        name="block",
    )(x, kt, v, *consts)
```
